```python
import jax, jax.numpy as jnp
from jax import lax
import numpy as np

D_MODEL = 4096
BATCH = 2
SEQ = 8192
DEPTH = 1

CHUNK = 64
Q_BLOCK = 128
SB_HEADS = 16
SB_HEAD_DIM = 128
SB_WIDTH = SB_HEADS * SB_HEAD_DIM
RET_HEADS = 8
RET_HEAD_DIM = 256
RET_WIDTH = RET_HEADS * RET_HEAD_DIM
MIX_WIDTH = SB_WIDTH + RET_WIDTH
IN_WIDTH = 3 * SB_WIDTH + 4 * RET_WIDTH
D_FF = 4 * D_MODEL
ROPE_BASE = 10000.0
EPS = 1e-6

kernel_name = "hymba_stickbreak_retention_block"


def rms_norm(x, g):
    xf = x.astype(jnp.float32)
    y = xf * lax.rsqrt(jnp.mean(xf * xf, axis=-1, keepdims=True) + EPS)
    return (y * g.astype(jnp.float32)).astype(x.dtype)


def to_heads(t, n_heads):
    b, s, w = t.shape
    return t.reshape(b, s, n_heads, w // n_heads).transpose(0, 2, 1, 3)


def from_heads(t):
    b, h, s, d = t.shape
    return t.transpose(0, 2, 1, 3).reshape(b, s, h * d)


def stick_breaking_attention(q, k, v):
    b, h, s, dh = q.shape
    scale = dh ** -0.5
    outs = []
    for start in range(0, s, Q_BLOCK):
        end = start + Q_BLOCK
        qb = q[:, :, start:end]
        kp = k[:, :, :end]
        vp = v[:, :, :end]
        z = jnp.einsum('bhqd,bhkd->bhqk', qb, kp).astype(jnp.float32) * scale
        t_pos = start + jnp.arange(Q_BLOCK)[:, None]
        s_pos = jnp.arange(end)[None, :]
        mask = s_pos < t_pos
        log_beta = jax.nn.log_sigmoid(z)
        log_one_minus = jnp.where(mask, log_beta - z, 0.0)
        tail = lax.cumsum(log_one_minus, axis=3, reverse=True) - log_one_minus
        a = jnp.where(mask, jnp.exp(log_beta + tail), 0.0)
        outs.append(jnp.einsum('bhqk,bhkd->bhqd', a.astype(vp.dtype), vp))
    return jnp.concatenate(outs, axis=2)


def rotary(x, pos):
    half = x.shape[-1] // 2
    inv_freq = ROPE_BASE ** (-jnp.arange(half, dtype=jnp.float32) / half)
    ang = pos.astype(jnp.float32)[:, None] * inv_freq[None, :]
    cos, sin = jnp.cos(ang), jnp.sin(ang)
    x1, x2 = x[..., :half], x[..., half:]
    return jnp.concatenate([x1 * cos - x2 * sin, x2 * cos + x1 * sin], axis=-1)


def retention_chunkwise(q, k, v):
    b, h, s, dk = q.shape
    dv = v.shape[-1]
    n = s // CHUNK
    log_gamma = jnp.log1p(-(2.0 ** (-5.0 - jnp.arange(h, dtype=jnp.float32))))
    idx = jnp.arange(CHUNK, dtype=jnp.float32)
    rel = idx[:, None] - idx[None, :]
    decay_mask = jnp.where(rel >= 0.0,
                           jnp.exp(log_gamma[:, None, None] * jnp.maximum(rel, 0.0)), 0.0)
    q_decay = jnp.exp(log_gamma[:, None] * (idx + 1.0))
    k_decay = jnp.exp(log_gamma[:, None] * (CHUNK - 1.0 - idx))
    chunk_decay = jnp.exp(log_gamma * CHUNK)

    qc = q.reshape(b, h, n, CHUNK, dk)
    kc = k.reshape(b, h, n, CHUNK, dk)
    vc = v.reshape(b, h, n, CHUNK, dv)
    scores = jnp.einsum('bhncd,bhnmd->bhncm', qc, kc) * decay_mask[None, :, None]
    inner = jnp.einsum('bhncm,bhnme->bhnce', scores, vc)

    def step(state, xs):
        q_i, k_i, v_i = xs
        cross = jnp.einsum('bhcd,bhde->bhce', q_i, state) * q_decay[None, :, :, None]
        state = state * chunk_decay[None, :, None, None] + jnp.einsum(
            'bhcd,bhce->bhde', k_i * k_decay[None, :, :, None], v_i)
        return state, cross

    xs = (jnp.moveaxis(qc, 2, 0), jnp.moveaxis(kc, 2, 0), jnp.moveaxis(vc, 2, 0))
    state0 = jnp.zeros((b, h, dk, dv), jnp.float32)
    _, cross = lax.scan(step, state0, xs)
    cross = jnp.moveaxis(cross, 0, 2)
    return (inner + cross).reshape(b, h, s, dv)


def head_group_norm(y, g):
    mu = jnp.mean(y, axis=-1, keepdims=True)
    yc = y - mu
    yn = yc * lax.rsqrt(jnp.mean(yc * yc, axis=-1, keepdims=True) + EPS)
    return from_heads(yn) * g.astype(jnp.float32)


def setup_inputs(seed: int = 0) -> dict:
    key = jax.random.key(seed)
    ks = jax.random.split(key, 10)
    f32 = jnp.float32

    def gain(k, n):
        return 1.0 + 0.02 * jax.random.normal(k, (DEPTH, n), f32)

    return {
        "x": jax.random.normal(ks[0], (BATCH, SEQ, D_MODEL), f32),
        "attn_norm_g": gain(ks[1], D_MODEL),
        "w_in": jax.random.normal(ks[2], (DEPTH, D_MODEL, IN_WIDTH), f32) * D_MODEL ** -0.5,
        "sb_norm_g": gain(ks[3], SB_WIDTH),
        "ret_norm_g": gain(ks[4], RET_WIDTH),
        "w_out": jax.random.normal(ks[5], (DEPTH, MIX_WIDTH, D_MODEL), f32) * MIX_WIDTH ** -0.5,
        "mlp_norm_g": gain(ks[6], D_MODEL),
        "w_up": jax.random.normal(ks[7], (DEPTH, D_MODEL, D_FF), f32) * D_MODEL ** -0.5,
        "w_down": jax.random.normal(ks[8], (DEPTH, D_FF, D_MODEL), f32) * D_FF ** -0.5,
        "final_norm_g": 1.0 + 0.02 * jax.random.normal(ks[9], (D_MODEL,), f32),
    }


def reference(x, attn_norm_g, w_in, sb_norm_g, ret_norm_g, w_out, mlp_norm_g, w_up, w_down,
              final_norm_g):
    s = x.shape[1]
    pos = jnp.arange(s, dtype=jnp.int32)
    for l in range(DEPTH):
        h = rms_norm(x, attn_norm_g[l])
        proj = h @ w_in[l]
        o = 0
        sb_q = proj[..., o:o + SB_WIDTH]; o += SB_WIDTH
        sb_k = proj[..., o:o + SB_WIDTH]; o += SB_WIDTH
        sb_v = proj[..., o:o + SB_WIDTH]; o += SB_WIDTH
        r_q = proj[..., o:o + RET_WIDTH]; o += RET_WIDTH
        r_k = proj[..., o:o + RET_WIDTH]; o += RET_WIDTH
        r_v = proj[..., o:o + RET_WIDTH]; o += RET_WIDTH
        r_g = proj[..., o:o + RET_WIDTH]

        sb = stick_breaking_attention(to_heads(sb_q, SB_HEADS), to_heads(sb_k, SB_HEADS),
                                      to_heads(sb_v, SB_HEADS))
        sb = rms_norm(from_heads(sb), sb_norm_g[l])

        rq = rotary(to_heads(r_q, RET_HEADS).astype(jnp.float32), pos)
        rk = rotary(to_heads(r_k, RET_HEADS).astype(jnp.float32), pos) * RET_HEAD_DIM ** -0.5
        rv = to_heads(r_v, RET_HEADS).astype(jnp.float32)
        ret = head_group_norm(retention_chunkwise(rq, rk, rv), ret_norm_g[l])
        ret = (jax.nn.silu(r_g.astype(jnp.float32)) * ret).astype(x.dtype)

        mixed = jnp.concatenate([sb.astype(x.dtype), ret], axis=-1)
        x = x + mixed @ w_out[l]

        h2 = rms_norm(x, mlp_norm_g[l])
        x = x + jnp.square(jax.nn.relu(h2 @ w_up[l])) @ w_down[l]
    return rms_norm(x, final_norm_g)
```

```python
import functools
import math

import jax
import jax.numpy as jnp
from jax import lax
from jax.experimental import pallas as pl
from jax.experimental.pallas import tpu as pltpu

F32 = jnp.float32
BF16 = jnp.bfloat16

SB_HEAD_DIM = 128
RET_HEAD_DIM = 256
ROPE_BASE = 10000.0
EPS = 1e-6

VMEM_LIMIT_BYTES = 56 * 1024 * 1024


def _params(semantics):
    return pltpu.CompilerParams(dimension_semantics=semantics,
                                vmem_limit_bytes=VMEM_LIMIT_BYTES)


def _tile(n, pref):
    t = pref
    while n % t:
        t //= 2
    return t


def _rms_scale(x):
    return lax.rsqrt(jnp.mean(x * x, axis=-1, keepdims=True) + EPS)


def _rope_kernel(inv_ref, cos_ref, sin_ref):
    rows, half = cos_ref.shape
    pos = lax.broadcasted_iota(jnp.int32, (rows, half), 0) + pl.program_id(0) * rows
    ang = pos.astype(F32) * inv_ref[...]
    cos_ref[...] = jnp.cos(ang)
    sin_ref[...] = jnp.sin(ang)


def _rope_table(seq, half):
    inv_freq = ROPE_BASE ** (-jnp.arange(half, dtype=F32) / half)
    rows = min(seq, 1024)
    return pl.pallas_call(
        _rope_kernel,
        grid=(seq // rows,),
        in_specs=[pl.BlockSpec((1, half), lambda i: (0, 0))],
        out_specs=[pl.BlockSpec((rows, half), lambda i: (i, 0))] * 2,
        out_shape=[jax.ShapeDtypeStruct((seq, half), F32)] * 2,
        compiler_params=_params(("arbitrary",)),
        name="rope_table",
    )(inv_freq.reshape(1, half))


def _norm_inproj_kernel(x_ref, g_ref, w_ref, cs_ref, o_ref, h_ref):
    @pl.when(pl.program_id(1) == 0)
    def _():
        x = x_ref[...]
        h_ref[...] = (x * _rms_scale(x) * g_ref[...]).astype(h_ref.dtype)

    acc = jnp.dot(h_ref[...], w_ref[...], preferred_element_type=F32)
    o_ref[...] = (acc * cs_ref[...]).astype(o_ref.dtype)


def _norm_inproj(x2d, g, w_bf16, colscale, tm, tn):
    m, d = x2d.shape
    n = w_bf16.shape[1]
    return pl.pallas_call(
        _norm_inproj_kernel,
        grid=(m // tm, n // tn),
        in_specs=[
            pl.BlockSpec((tm, d), lambda i, j: (i, 0)),
            pl.BlockSpec((1, d), lambda i, j: (0, 0)),
            pl.BlockSpec((d, tn), lambda i, j: (0, j)),
            pl.BlockSpec((1, tn), lambda i, j: (0, j)),
        ],
        out_specs=pl.BlockSpec((tm, tn), lambda i, j: (i, j)),
        out_shape=jax.ShapeDtypeStruct((m, n), BF16),
        scratch_shapes=[pltpu.VMEM((tm, d), BF16)],
        compiler_params=_params(("arbitrary", "arbitrary")),
        name="norm_inproj",
    )(x2d, g.reshape(1, d), w_bf16, colscale.reshape(1, n))


def _sb_attn_kernel(q_ref, k_ref, v_ref, o_ref, acc_ref, carry_ref, *, blk):
    qi = pl.program_id(2)
    q = q_ref[...]
    key_idx = lax.broadcasted_iota(jnp.int32, (blk, blk), 0)
    qry_idx = lax.broadcasted_iota(jnp.int32, (blk, blk), 1)
    neg_tri = jnp.where(qry_idx > key_idx, -1.0, 0.0).astype(BF16)

    def block(j, masked):
        start = pl.multiple_of(j * blk, blk)
        kb = k_ref[pl.ds(start, blk), :]
        vb = v_ref[pl.ds(start, blk), :]
        z = lax.dot_general(kb, q, (((1,), (1,)), ((), ())),
                            preferred_element_type=F32)
        sp = jnp.maximum(z, 0.0) + jnp.log(1.0 + jnp.exp(-jnp.abs(z)))
        log_beta = z - sp
        if masked:
            mask = key_idx < qry_idx
            sp = jnp.where(mask, sp, 0.0)
        sp_hi = sp.astype(BF16)
        sp_lo = (sp - sp_hi.astype(F32)).astype(BF16)
        tail = (jnp.dot(neg_tri, sp_hi, preferred_element_type=F32)
                + jnp.dot(neg_tri, sp_lo, preferred_element_type=F32))
        carry = carry_ref[...]
        a = jnp.exp(log_beta + tail + carry)
        if masked:
            a = jnp.where(mask, a, 0.0)
        acc_ref[...] += lax.dot_general(vb, a.astype(BF16), (((0,), (0,)), ((), ())),
                                        preferred_element_type=F32)
        carry_ref[...] = carry - jnp.sum(sp, axis=0, keepdims=True)

    acc_ref[...] = jnp.zeros_like(acc_ref)
    carry_ref[...] = jnp.zeros_like(carry_ref)
    block(qi, True)

    def body(t, c):
        block(qi - 1 - t, False)
        return c

    lax.fori_loop(0, qi, body, 0)
    o_ref[...] = acc_ref[...].T


def _sb_attention(proj, batch, seq, n_heads, blk):
    m = proj.shape[0]
    dh = SB_HEAD_DIM
    nq = seq // blk
    return pl.pallas_call(
        functools.partial(_sb_attn_kernel, blk=blk),
        grid=(batch, n_heads, nq),
        in_specs=[
            pl.BlockSpec((blk, dh), lambda b, h, i: (b * nq + i, h)),
            pl.BlockSpec((seq, dh), lambda b, h, i: (b, n_heads + h)),
            pl.BlockSpec((seq, dh), lambda b, h, i: (b, 2 * n_heads + h)),
        ],
        out_specs=pl.BlockSpec((blk, dh), lambda b, h, i: (b * nq + i, h)),
        out_shape=jax.ShapeDtypeStruct((m, n_heads * dh), F32),
        scratch_shapes=[pltpu.VMEM((dh, blk), F32), pltpu.VMEM((1, blk), F32)],
        compiler_params=_params(("arbitrary", "arbitrary", "arbitrary")),
        name="sb_attention",
    )(proj, proj, proj)


def _retention_kernel(lg_ref, q_ref, k_ref, v_ref, gate_ref, cos_ref, sin_ref, g_ref,
                      o_ref, state_ref, *, chunk):
    h = pl.program_id(1)
    c_idx = pl.program_id(2)
    half = RET_HEAD_DIM // 2

    @pl.when(c_idx == 0)
    def _():
        state_ref[...] = jnp.zeros_like(state_ref)

    log_gamma = lg_ref[h]
    cos = cos_ref[...]
    sin = sin_ref[...]

    def rotary(x):
        x1, x2 = x[:, :half], x[:, half:]
        return jnp.concatenate([x1 * cos - x2 * sin, x2 * cos + x1 * sin], axis=-1)

    q = rotary(q_ref[...].astype(F32))
    k = rotary(k_ref[...].astype(F32))
    v = v_ref[...]

    row = lax.broadcasted_iota(jnp.int32, (chunk, chunk), 0)
    col = lax.broadcasted_iota(jnp.int32, (chunk, chunk), 1)
    rel = (row - col).astype(F32)
    decay_mask = jnp.where(rel >= 0.0, jnp.exp(log_gamma * jnp.maximum(rel, 0.0)), 0.0)
    idx = lax.broadcasted_iota(jnp.int32, (chunk, 1), 0).astype(F32)
    q_decay = jnp.exp(log_gamma * (idx + 1.0))
    k_decay = jnp.exp(log_gamma * (chunk - 1.0 - idx))
    chunk_decay = jnp.exp(log_gamma * chunk)

    q_b = q.astype(BF16)
    k_b = k.astype(BF16)
    scores = lax.dot_general(q_b, k_b, (((1,), (1,)), ((), ())),
                             preferred_element_type=F32) * decay_mask
    inner = jnp.dot(scores.astype(BF16), v, preferred_element_type=F32)
    state = state_ref[...]
    cross = jnp.dot(q_b, state.astype(BF16), preferred_element_type=F32) * q_decay
    state_ref[...] = state * chunk_decay + lax.dot_general(
        (k * k_decay).astype(BF16), v, (((0,), (0,)), ((), ())),
        preferred_element_type=F32)

    y = inner + cross
    mu = jnp.mean(y, axis=-1, keepdims=True)
    yc = y - mu
    yn = yc * lax.rsqrt(jnp.mean(yc * yc, axis=-1, keepdims=True) + EPS)
    ret = yn * g_ref[...]
    gate = gate_ref[...].astype(F32)
    silu = gate * (1.0 / (1.0 + jnp.exp(-gate)))
    o_ref[...] = (silu * ret).astype(o_ref.dtype)


def _retention(proj, cos, sin, ret_g, batch, seq, n_heads, col0, chunk):
    m = proj.shape[0]
    dh = RET_HEAD_DIM
    nc = seq // chunk
    log_gamma = jnp.log1p(-(2.0 ** (-5.0 - jnp.arange(n_heads, dtype=F32))))

    def col_spec(group):
        base = col0 + group * n_heads
        return pl.BlockSpec((chunk, dh), lambda b, h, c, lg: (b * nc + c, base + h))

    grid_spec = pltpu.PrefetchScalarGridSpec(
        num_scalar_prefetch=1,
        grid=(batch, n_heads, nc),
        in_specs=[
            col_spec(0), col_spec(1), col_spec(2), col_spec(3),
            pl.BlockSpec((chunk, dh // 2), lambda b, h, c, lg: (c, 0)),
            pl.BlockSpec((chunk, dh // 2), lambda b, h, c, lg: (c, 0)),
            pl.BlockSpec((1, dh), lambda b, h, c, lg: (0, h)),
        ],
        out_specs=pl.BlockSpec((chunk, dh), lambda b, h, c, lg: (b * nc + c, h)),
        scratch_shapes=[pltpu.VMEM((dh, dh), F32)],
    )
    return pl.pallas_call(
        functools.partial(_retention_kernel, chunk=chunk),
        grid_spec=grid_spec,
        out_shape=jax.ShapeDtypeStruct((m, n_heads * dh), BF16),
        compiler_params=_params(("arbitrary", "arbitrary", "arbitrary")),
        name="retention",
    )(log_gamma, proj, proj, proj, proj, cos, sin, ret_g.reshape(1, n_heads * dh))


def _outproj_kernel(sb_ref, ret_ref, g_ref, w_ref, x_ref, o_ref, mix_ref):
    sbw = sb_ref.shape[1]

    @pl.when(pl.program_id(1) == 0)
    def _():
        sb = sb_ref[...]
        mix_ref[:, :sbw] = (sb * _rms_scale(sb) * g_ref[...]).astype(mix_ref.dtype)
        mix_ref[:, sbw:] = ret_ref[...]

    o_ref[...] = x_ref[...] + jnp.dot(mix_ref[...], w_ref[...], preferred_element_type=F32)


def _outproj(sb, ret, sb_g, w_bf16, x2d, tm, tn):
    m, d = x2d.shape
    sbw = sb.shape[1]
    rw = ret.shape[1]
    return pl.pallas_call(
        _outproj_kernel,
        grid=(m // tm, d // tn),
        in_specs=[
            pl.BlockSpec((tm, sbw), lambda i, j: (i, 0)),
            pl.BlockSpec((tm, rw), lambda i, j: (i, 0)),
            pl.BlockSpec((1, sbw), lambda i, j: (0, 0)),
            pl.BlockSpec((sbw + rw, tn), lambda i, j: (0, j)),
            pl.BlockSpec((tm, tn), lambda i, j: (i, j)),
        ],
        out_specs=pl.BlockSpec((tm, tn), lambda i, j: (i, j)),
        out_shape=jax.ShapeDtypeStruct((m, d), F32),
        scratch_shapes=[pltpu.VMEM((tm, sbw + rw), BF16)],
        compiler_params=_params(("arbitrary", "arbitrary")),
        name="outproj",
    )(sb, ret, sb_g.reshape(1, sbw), w_bf16, x2d)


def _mlp_kernel(x_ref, g_ref, wu_ref, wd_ref, gf_ref, o_ref, h_ref):
    f = pl.program_id(1)

    @pl.when(f == 0)
    def _():
        x = x_ref[...]
        h_ref[...] = (x * _rms_scale(x) * g_ref[...]).astype(h_ref.dtype)
        o_ref[...] = jnp.zeros_like(o_ref)

    u = jnp.dot(h_ref[...], wu_ref[...], preferred_element_type=F32)
    a = jnp.square(jnp.maximum(u, 0.0)).astype(BF16)
    o_ref[...] += jnp.dot(a, wd_ref[...], preferred_element_type=F32)

    @pl.when(f == pl.num_programs(1) - 1)
    def _():
        x2 = x_ref[...] + o_ref[...]
        o_ref[...] = x2 * _rms_scale(x2) * gf_ref[...]


def _mlp(x1, g, wu_bf16, wd_bf16, gf, tm, tf):
    m, d = x1.shape
    dff = wu_bf16.shape[1]
    return pl.pallas_call(
        _mlp_kernel,
        grid=(m // tm, dff // tf),
        in_specs=[
            pl.BlockSpec((tm, d), lambda i, f: (i, 0), pipeline_mode=pl.Buffered(1)),
            pl.BlockSpec((1, d), lambda i, f: (0, 0)),
            pl.BlockSpec((d, tf), lambda i, f: (0, f)),
            pl.BlockSpec((tf, d), lambda i, f: (f, 0)),
            pl.BlockSpec((1, d), lambda i, f: (0, 0)),
        ],
        out_specs=pl.BlockSpec((tm, d), lambda i, f: (i, 0), pipeline_mode=pl.Buffered(1)),
        out_shape=jax.ShapeDtypeStruct((m, d), F32),
        scratch_shapes=[pltpu.VMEM((tm, d), BF16)],
        compiler_params=_params(("arbitrary", "arbitrary")),
        name="mlp",
    )(x1, g.reshape(1, d), wu_bf16, wd_bf16, gf.reshape(1, d))


def kernel(x, attn_norm_g, w_in, sb_norm_g, ret_norm_g, w_out, mlp_norm_g, w_up, w_down,
           final_norm_g):
    batch, seq, d = x.shape
    depth = w_in.shape[0]
    sb_width = sb_norm_g.shape[-1]
    ret_width = ret_norm_g.shape[-1]
    sb_heads = sb_width // SB_HEAD_DIM
    ret_heads = ret_width // RET_HEAD_DIM
    in_width = w_in.shape[-1]
    assert in_width == 3 * sb_width + 4 * ret_width
    m = batch * seq

    tm = _tile(m, 512)
    blk = _tile(seq, 256)
    chunk = _tile(seq, 256)
    cos, sin = _rope_table(seq, RET_HEAD_DIM // 2)

    colscale = jnp.ones((in_width,), F32)
    colscale = colscale.at[:sb_width].set(SB_HEAD_DIM ** -0.5)
    rk0 = 3 * sb_width + ret_width
    colscale = colscale.at[rk0:rk0 + ret_width].set(RET_HEAD_DIM ** -0.5)

    xf = x.reshape(m, d)
    for l in range(depth):
        proj = _norm_inproj(xf, attn_norm_g[l], w_in[l].astype(BF16), colscale,
                            tm, _tile(in_width, 1024))
        sb = _sb_attention(proj, batch, seq, sb_heads, blk)
        ret = _retention(proj, cos, sin, ret_norm_g[l], batch, seq, ret_heads,
                         3 * sb_width // RET_HEAD_DIM, chunk)
        x1 = _outproj(sb, ret, sb_norm_g[l], w_out[l].astype(BF16), xf, tm, _tile(d, 1024))
        last = l == depth - 1
        assert last, "final norm is fused into the last layer's MLP kernel"
        xf = _mlp(x1, mlp_norm_g[l], w_up[l].astype(BF16), w_down[l].astype(BF16),
                  final_norm_g, tm, _tile(w_up.shape[-1], 512))
    return xf.reshape(batch, seq, d)
```

```python
import functools
import math

import jax
import jax.numpy as jnp
from jax import lax
from jax.experimental import pallas as pl
from jax.experimental.pallas import tpu as pltpu

F32 = jnp.float32
BF16 = jnp.bfloat16

SB_HEAD_DIM = 128
RET_HEAD_DIM = 256
ROPE_BASE = 10000.0
EPS = 1e-6

VMEM_LIMIT_BYTES = 56 * 1024 * 1024


def _params(semantics):
    return pltpu.CompilerParams(dimension_semantics=semantics,
                                vmem_limit_bytes=VMEM_LIMIT_BYTES)


def _tile(n, pref):
    t = pref
    while n % t:
        t //= 2
    return t


def _rms_scale(x):
    return lax.rsqrt(jnp.mean(x * x, axis=-1, keepdims=True) + EPS)


def _rope_kernel(inv_ref, cos_ref, sin_ref):
    rows, half = cos_ref.shape
    pos = lax.broadcasted_iota(jnp.int32, (rows, half), 0) + pl.program_id(0) * rows
    ang = pos.astype(F32) * inv_ref[...]
    cos_ref[...] = jnp.cos(ang)
    sin_ref[...] = jnp.sin(ang)


def _rope_table(seq, half):
    inv_freq = ROPE_BASE ** (-jnp.arange(half, dtype=F32) / half)
    rows = min(seq, 1024)
    return pl.pallas_call(
        _rope_kernel,
        grid=(seq // rows,),
        in_specs=[pl.BlockSpec((1, half), lambda i: (0, 0))],
        out_specs=[pl.BlockSpec((rows, half), lambda i: (i, 0))] * 2,
        out_shape=[jax.ShapeDtypeStruct((seq, half), F32)] * 2,
        compiler_params=_params(("arbitrary",)),
        name="rope_table",
    )(inv_freq.reshape(1, half))


def _norm_inproj_kernel(x_ref, g_ref, w_ref, cs_ref, o_ref, h_ref):
    @pl.when(pl.program_id(1) == 0)
    def _():
        x = x_ref[...]
        h_ref[...] = (x * _rms_scale(x) * g_ref[...]).astype(h_ref.dtype)

    acc = jnp.dot(h_ref[...], w_ref[...], preferred_element_type=F32)
    o_ref[...] = (acc * cs_ref[...]).astype(o_ref.dtype)


def _norm_inproj(x2d, g, w_bf16, colscale, tm, tn):
    m, d = x2d.shape
    n = w_bf16.shape[1]
    return pl.pallas_call(
        _norm_inproj_kernel,
        grid=(m // tm, n // tn),
        in_specs=[
            pl.BlockSpec((tm, d), lambda i, j: (i, 0)),
            pl.BlockSpec((1, d), lambda i, j: (0, 0)),
            pl.BlockSpec((d, tn), lambda i, j: (0, j)),
            pl.BlockSpec((1, tn), lambda i, j: (0, j)),
        ],
        out_specs=pl.BlockSpec((tm, tn), lambda i, j: (i, j)),
        out_shape=jax.ShapeDtypeStruct((m, n), BF16),
        scratch_shapes=[pltpu.VMEM((tm, d), BF16)],
        compiler_params=_params(("arbitrary", "arbitrary")),
        name="norm_inproj",
    )(x2d, g.reshape(1, d), w_bf16, colscale.reshape(1, n))


LOG2E = 1.4426950408889634


def _softplus(z):
    return jnp.maximum(z, 0.0) + jnp.log(1.0 + jnp.exp2(jnp.abs(z) * (-LOG2E)))


NEG_BIG = -1e30
SB_STAGES = 5
SB_LANES = 2


def _sb_attn_kernel(q_ref, k_ref, v_ref, o_ref, vt_ref, acc_ref, carry_ref, ntri_ref,
                    z_ref, sp_ref, tail_ref, a_ref, crow_ref, *, blk, nq):
    key_idx = lax.broadcasted_iota(jnp.int32, (blk, blk), 0)
    qry_idx = lax.broadcasted_iota(jnp.int32, (blk, blk), 1)
    ntri_ref[...] = jnp.where(qry_idx >= key_idx, -1.0, 0.0).astype(BF16)

    def rows(i):
        return pl.ds(pl.multiple_of(i * blk, blk), blk)

    def fill_vt(jb, c):
        vt_ref[jb] = v_ref[rows(jb), :].T
        return c

    lax.fori_loop(0, nq, fill_vt, 0)
    acc_ref[...] = jnp.zeros_like(acc_ref)
    carry_ref[...] = jnp.zeros_like(carry_ref)

    def run_pipeline(n_items, first, advance, masked):
        for ref in (z_ref, sp_ref, tail_ref, a_ref):
            ref[...] = jnp.zeros_like(ref)
        crow_ref[...] = jnp.full_like(crow_ref, NEG_BIG)
        mask = key_idx < qry_idx if masked else None
        ring = SB_STAGES - 1

        def body(t, state):
            nxt, slot_b, slot_c, slot_d, slot_e = state
            slot_w = lax.rem(t, ring)
            slot_b_z = lax.rem(t + ring - 1, ring)
            slot_d_z = lax.rem(t + ring - 3, ring)

            for u in range(SB_LANES):
                q_e, j_e, _ = slot_e[u]
                acc_ref[q_e] += jnp.dot(vt_ref[j_e], a_ref[u], preferred_element_type=F32)

            for u in range(SB_LANES):
                arg = z_ref[slot_d_z * SB_LANES + u] + tail_ref[u] + crow_ref[u]
                a = jnp.exp2(arg * LOG2E)
                if masked:
                    a = jnp.where(mask, a, 0.0)
                a_ref[u] = a.astype(BF16)

            for u in range(SB_LANES):
                q_c, _, valid_c = slot_c[u]
                tail = jnp.dot(ntri_ref[...], sp_ref[u], preferred_element_type=F32)
                tail_ref[u] = tail
                carry = carry_ref[q_c]
                crow_ref[u] = carry + jnp.where(valid_c > 0, 0.0, NEG_BIG)
                carry_ref[q_c] = carry + tail[0:1, :] * valid_c.astype(F32)

            for u in range(SB_LANES):
                sp = _softplus(z_ref[slot_b_z * SB_LANES + u])
                if masked:
                    sp = jnp.where(mask, sp, 0.0)
                sp_ref[u] = sp.astype(BF16)

            issued = []
            for u in range(SB_LANES):
                q_n, j_n = nxt
                valid_a = (q_n < nq).astype(jnp.int32)
                q_a = jnp.minimum(q_n, nq - 1)
                j_a = jnp.minimum(j_n, nq - 1)
                z_ref[slot_w * SB_LANES + u] = lax.dot_general(
                    k_ref[rows(j_a), :], q_ref[rows(q_a), :],
                    (((1,), (1,)), ((), ())), preferred_element_type=F32)
                issued.append((q_a, j_a, valid_a))
                nxt = advance(q_n, j_n)
            return (nxt, tuple(issued), slot_b, slot_c, slot_d)

        idle = tuple((jnp.int32(0), jnp.int32(0), jnp.int32(0)) for _ in range(SB_LANES))
        n_iter = -(-n_items // SB_LANES) + SB_STAGES - 1
        lax.fori_loop(0, n_iter, body, (first, idle, idle, idle, idle))

    run_pipeline(nq, (jnp.int32(0), jnp.int32(0)), lambda q, j: (q + 1, j + 1), masked=True)

    def advance(q, j):
        wrap = j == 0
        return jnp.where(wrap, q + 1, q), jnp.where(wrap, q, j - 1)

    run_pipeline(nq * (nq - 1) // 2, (jnp.int32(1), jnp.int32(0)), advance, masked=False)

    def write_out(qb, c):
        o_ref[rows(qb), :] = acc_ref[qb].T
        return c

    lax.fori_loop(0, nq, write_out, 0)


def _sb_attention(proj, batch, seq, n_heads, blk):
    m = proj.shape[0]
    dh = SB_HEAD_DIM
    nq = seq // blk
    return pl.pallas_call(
        functools.partial(_sb_attn_kernel, blk=blk, nq=nq),
        grid=(batch, n_heads),
        in_specs=[
            pl.BlockSpec((seq, dh), lambda b, h: (b, h)),
            pl.BlockSpec((seq, dh), lambda b, h: (b, n_heads + h)),
            pl.BlockSpec((seq, dh), lambda b, h: (b, 2 * n_heads + h)),
        ],
        out_specs=pl.BlockSpec((seq, dh), lambda b, h: (b, h)),
        out_shape=jax.ShapeDtypeStruct((m, n_heads * dh), F32),
        scratch_shapes=[
            pltpu.VMEM((nq, dh, blk), BF16),
            pltpu.VMEM((nq, dh, blk), F32),
            pltpu.VMEM((nq, 1, blk), F32),
            pltpu.VMEM((blk, blk), BF16),
            pltpu.VMEM(((SB_STAGES - 1) * SB_LANES, blk, blk), F32),
            pltpu.VMEM((SB_LANES, blk, blk), BF16),
            pltpu.VMEM((SB_LANES, blk, blk), F32),
            pltpu.VMEM((SB_LANES, blk, blk), BF16),
            pltpu.VMEM((2 * SB_LANES, 1, blk), F32),
        ],
        compiler_params=_params(("arbitrary", "arbitrary")),
        name="sb_attention",
    )(proj, proj, proj)


def _retention_kernel(lg_ref, q_ref, k_ref, v_ref, gate_ref, cos_ref, sin_ref, g_ref,
                      o_ref, state_ref, *, chunk):
    h = pl.program_id(1)
    c_idx = pl.program_id(2)
    half = RET_HEAD_DIM // 2

    @pl.when(c_idx == 0)
    def _():
        state_ref[...] = jnp.zeros_like(state_ref)

    log_gamma = lg_ref[h]
    cos = cos_ref[...]
    sin = sin_ref[...]

    def rotary(x):
        x1, x2 = x[:, :half], x[:, half:]
        return jnp.concatenate([x1 * cos - x2 * sin, x2 * cos + x1 * sin], axis=-1)

    q = rotary(q_ref[...].astype(F32))
    k = rotary(k_ref[...].astype(F32))
    v = v_ref[...]

    row = lax.broadcasted_iota(jnp.int32, (chunk, chunk), 0)
    col = lax.broadcasted_iota(jnp.int32, (chunk, chunk), 1)
    rel = (row - col).astype(F32)
    decay_mask = jnp.where(rel >= 0.0, jnp.exp(log_gamma * jnp.maximum(rel, 0.0)), 0.0)
    idx = lax.broadcasted_iota(jnp.int32, (chunk, 1), 0).astype(F32)
    q_decay = jnp.exp(log_gamma * (idx + 1.0))
    k_decay = jnp.exp(log_gamma * (chunk - 1.0 - idx))
    chunk_decay = jnp.exp(log_gamma * chunk)

    q_b = q.astype(BF16)
    k_b = k.astype(BF16)
    scores = lax.dot_general(q_b, k_b, (((1,), (1,)), ((), ())),
                             preferred_element_type=F32) * decay_mask
    inner = jnp.dot(scores.astype(BF16), v, preferred_element_type=F32)
    state = state_ref[...]
    cross = jnp.dot(q_b, state.astype(BF16), preferred_element_type=F32) * q_decay
    state_ref[...] = state * chunk_decay + lax.dot_general(
        (k * k_decay).astype(BF16), v, (((0,), (0,)), ((), ())),
        preferred_element_type=F32)

    y = inner + cross
    mu = jnp.mean(y, axis=-1, keepdims=True)
    yc = y - mu
    yn = yc * lax.rsqrt(jnp.mean(yc * yc, axis=-1, keepdims=True) + EPS)
    ret = yn * g_ref[...]
    gate = gate_ref[...].astype(F32)
    silu = gate * (1.0 / (1.0 + jnp.exp(-gate)))
    o_ref[...] = (silu * ret).astype(o_ref.dtype)


def _retention(proj, cos, sin, ret_g, batch, seq, n_heads, col0, chunk):
    m = proj.shape[0]
    dh = RET_HEAD_DIM
    nc = seq // chunk
    log_gamma = jnp.log1p(-(2.0 ** (-5.0 - jnp.arange(n_heads, dtype=F32))))

    def col_spec(group):
        base = col0 + group * n_heads
        return pl.BlockSpec((chunk, dh), lambda b, h, c, lg: (b * nc + c, base + h))

    grid_spec = pltpu.PrefetchScalarGridSpec(
        num_scalar_prefetch=1,
        grid=(batch, n_heads, nc),
        in_specs=[
            col_spec(0), col_spec(1), col_spec(2), col_spec(3),
            pl.BlockSpec((chunk, dh // 2), lambda b, h, c, lg: (c, 0)),
            pl.BlockSpec((chunk, dh // 2), lambda b, h, c, lg: (c, 0)),
            pl.BlockSpec((1, dh), lambda b, h, c, lg: (0, h)),
        ],
        out_specs=pl.BlockSpec((chunk, dh), lambda b, h, c, lg: (b * nc + c, h)),
        scratch_shapes=[pltpu.VMEM((dh, dh), F32)],
    )
    return pl.pallas_call(
        functools.partial(_retention_kernel, chunk=chunk),
        grid_spec=grid_spec,
        out_shape=jax.ShapeDtypeStruct((m, n_heads * dh), BF16),
        compiler_params=_params(("arbitrary", "arbitrary", "arbitrary")),
        name="retention",
    )(log_gamma, proj, proj, proj, proj, cos, sin, ret_g.reshape(1, n_heads * dh))


def _outproj_kernel(sb_ref, ret_ref, g_ref, w_ref, x_ref, o_ref, mix_ref):
    sbw = sb_ref.shape[1]

    @pl.when(pl.program_id(1) == 0)
    def _():
        sb = sb_ref[...]
        mix_ref[:, :sbw] = (sb * _rms_scale(sb) * g_ref[...]).astype(mix_ref.dtype)
        mix_ref[:, sbw:] = ret_ref[...]

    o_ref[...] = x_ref[...] + jnp.dot(mix_ref[...], w_ref[...], preferred_element_type=F32)


def _outproj(sb, ret, sb_g, w_bf16, x2d, tm, tn):
    m, d = x2d.shape
    sbw = sb.shape[1]
    rw = ret.shape[1]
    return pl.pallas_call(
        _outproj_kernel,
        grid=(m // tm, d // tn),
        in_specs=[
            pl.BlockSpec((tm, sbw), lambda i, j: (i, 0)),
            pl.BlockSpec((tm, rw), lambda i, j: (i, 0)),
            pl.BlockSpec((1, sbw), lambda i, j: (0, 0)),
            pl.BlockSpec((sbw + rw, tn), lambda i, j: (0, j)),
            pl.BlockSpec((tm, tn), lambda i, j: (i, j)),
        ],
        out_specs=pl.BlockSpec((tm, tn), lambda i, j: (i, j)),
        out_shape=jax.ShapeDtypeStruct((m, d), F32),
        scratch_shapes=[pltpu.VMEM((tm, sbw + rw), BF16)],
        compiler_params=_params(("arbitrary", "arbitrary")),
        name="outproj",
    )(sb, ret, sb_g.reshape(1, sbw), w_bf16, x2d)


def _mlp_kernel(x_ref, g_ref, wu_ref, wd_ref, gf_ref, o_ref, h_ref):
    f = pl.program_id(1)

    @pl.when(f == 0)
    def _():
        x = x_ref[...]
        h_ref[...] = (x * _rms_scale(x) * g_ref[...]).astype(h_ref.dtype)
        o_ref[...] = jnp.zeros_like(o_ref)

    u = jnp.dot(h_ref[...], wu_ref[...], preferred_element_type=F32)
    a = jnp.square(jnp.maximum(u, 0.0)).astype(BF16)
    o_ref[...] += jnp.dot(a, wd_ref[...], preferred_element_type=F32)

    @pl.when(f == pl.num_programs(1) - 1)
    def _():
        x2 = x_ref[...] + o_ref[...]
        o_ref[...] = x2 * _rms_scale(x2) * gf_ref[...]


def _mlp(x1, g, wu_bf16, wd_bf16, gf, tm, tf):
    m, d = x1.shape
    dff = wu_bf16.shape[1]
    return pl.pallas_call(
        _mlp_kernel,
        grid=(m // tm, dff // tf),
        in_specs=[
            pl.BlockSpec((tm, d), lambda i, f: (i, 0), pipeline_mode=pl.Buffered(1)),
            pl.BlockSpec((1, d), lambda i, f: (0, 0)),
            pl.BlockSpec((d, tf), lambda i, f: (0, f)),
            pl.BlockSpec((tf, d), lambda i, f: (f, 0)),
            pl.BlockSpec((1, d), lambda i, f: (0, 0)),
        ],
        out_specs=pl.BlockSpec((tm, d), lambda i, f: (i, 0), pipeline_mode=pl.Buffered(1)),
        out_shape=jax.ShapeDtypeStruct((m, d), F32),
        scratch_shapes=[pltpu.VMEM((tm, d), BF16)],
        compiler_params=_params(("arbitrary", "arbitrary")),
        name="mlp",
    )(x1, g.reshape(1, d), wu_bf16, wd_bf16, gf.reshape(1, d))


def kernel(x, attn_norm_g, w_in, sb_norm_g, ret_norm_g, w_out, mlp_norm_g, w_up, w_down,
           final_norm_g):
    batch, seq, d = x.shape
    depth = w_in.shape[0]
    sb_width = sb_norm_g.shape[-1]
    ret_width = ret_norm_g.shape[-1]
    sb_heads = sb_width // SB_HEAD_DIM
    ret_heads = ret_width // RET_HEAD_DIM
    in_width = w_in.shape[-1]
    assert in_width == 3 * sb_width + 4 * ret_width
    m = batch * seq

    tm = _tile(m, 512)
    blk = _tile(seq, 256)
    chunk = _tile(seq, 256)
    cos, sin = _rope_table(seq, RET_HEAD_DIM // 2)

    colscale = jnp.ones((in_width,), F32)
    colscale = colscale.at[:sb_width].set(SB_HEAD_DIM ** -0.5)
    rk0 = 3 * sb_width + ret_width
    colscale = colscale.at[rk0:rk0 + ret_width].set(RET_HEAD_DIM ** -0.5)

    xf = x.reshape(m, d)
    for l in range(depth):
        proj = _norm_inproj(xf, attn_norm_g[l], w_in[l].astype(BF16), colscale,
                            tm, _tile(in_width, 1024))
        sb = _sb_attention(proj, batch, seq, sb_heads, blk)
        ret = _retention(proj, cos, sin, ret_norm_g[l], batch, seq, ret_heads,
                         3 * sb_width // RET_HEAD_DIM, chunk)
        x1 = _outproj(sb, ret, sb_norm_g[l], w_out[l].astype(BF16), xf, tm, _tile(d, 1024))
        last = l == depth - 1
        assert last, "final norm is fused into the last layer's MLP kernel"
        xf = _mlp(x1, mlp_norm_g[l], w_up[l].astype(BF16), w_down[l].astype(BF16),
                  final_norm_g, tm, _tile(w_up.shape[-1], 512))
    return xf.reshape(batch, seq, d)
```

```python
import functools
import math

import jax
import jax.numpy as jnp
from jax import lax
from jax.experimental import pallas as pl
from jax.experimental.pallas import tpu as pltpu

F32 = jnp.float32
BF16 = jnp.bfloat16

SB_HEAD_DIM = 128
RET_HEAD_DIM = 256
ROPE_BASE = 10000.0
EPS = 1e-6

VMEM_LIMIT_BYTES = 56 * 1024 * 1024


def _params(semantics):
    return pltpu.CompilerParams(dimension_semantics=semantics,
                                vmem_limit_bytes=VMEM_LIMIT_BYTES)


def _tile(n, pref):
    t = pref
    while n % t:
        t //= 2
    return t


def _rms_scale(x):
    return lax.rsqrt(jnp.mean(x * x, axis=-1, keepdims=True) + EPS)


def _rope_kernel(inv_ref, cos_ref, sin_ref):
    rows, half = cos_ref.shape
    pos = lax.broadcasted_iota(jnp.int32, (rows, half), 0) + pl.program_id(0) * rows
    ang = pos.astype(F32) * inv_ref[...]
    cos_ref[...] = jnp.cos(ang)
    sin_ref[...] = jnp.sin(ang)


def _rope_table(seq, half):
    inv_freq = ROPE_BASE ** (-jnp.arange(half, dtype=F32) / half)
    rows = min(seq, 1024)
    return pl.pallas_call(
        _rope_kernel,
        grid=(seq // rows,),
        in_specs=[pl.BlockSpec((1, half), lambda i: (0, 0))],
        out_specs=[pl.BlockSpec((rows, half), lambda i: (i, 0))] * 2,
        out_shape=[jax.ShapeDtypeStruct((seq, half), F32)] * 2,
        compiler_params=_params(("arbitrary",)),
        name="rope_table",
    )(inv_freq.reshape(1, half))


def _norm_inproj_kernel(x_ref, g_ref, w_ref, cs_ref, o_ref, h_ref):
    @pl.when(pl.program_id(1) == 0)
    def _():
        x = x_ref[...]
        h_ref[...] = (x * _rms_scale(x) * g_ref[...]).astype(h_ref.dtype)

    acc = jnp.dot(h_ref[...], w_ref[...], preferred_element_type=F32)
    o_ref[...] = (acc * cs_ref[...]).astype(o_ref.dtype)


def _norm_inproj(x2d, g, w_bf16, colscale, tm, tn):
    m, d = x2d.shape
    n = w_bf16.shape[1]
    return pl.pallas_call(
        _norm_inproj_kernel,
        grid=(m // tm, n // tn),
        in_specs=[
            pl.BlockSpec((tm, d), lambda i, j: (i, 0)),
            pl.BlockSpec((1, d), lambda i, j: (0, 0)),
            pl.BlockSpec((d, tn), lambda i, j: (0, j)),
            pl.BlockSpec((1, tn), lambda i, j: (0, j)),
        ],
        out_specs=pl.BlockSpec((tm, tn), lambda i, j: (i, j)),
        out_shape=jax.ShapeDtypeStruct((m, n), BF16),
        scratch_shapes=[pltpu.VMEM((tm, d), BF16)],
        compiler_params=_params(("arbitrary", "arbitrary")),
        name="norm_inproj",
    )(x2d, g.reshape(1, d), w_bf16, colscale.reshape(1, n))


LOG2E = 1.4426950408889634


def _softplus(z):
    return jnp.maximum(z, 0.0) + jnp.log(1.0 + jnp.exp2(jnp.abs(z) * (-LOG2E)))


NEG_BIG = -1e30
SB_STAGES = 3
SB_LANES = 4


def _sb_attn_kernel(q_ref, k_ref, v_ref, o_ref, vt_ref, acc_ref, carry_ref, ntri_ref,
                    z_ref, tail_ref, crow_ref, *, bq, bk, nqb, nkb):
    kpq = bq // bk
    tri_row = lax.broadcasted_iota(jnp.int32, (bk, bk), 0)
    tri_col = lax.broadcasted_iota(jnp.int32, (bk, bk), 1)
    ntri_ref[...] = jnp.where(tri_col >= tri_row, -1.0, 0.0).astype(BF16)
    key_idx = lax.broadcasted_iota(jnp.int32, (bk, bq), 0)
    qry_idx = lax.broadcasted_iota(jnp.int32, (bk, bq), 1)

    def krows(j):
        return pl.ds(pl.multiple_of(j * bk, bk), bk)

    def qrows(q):
        return pl.ds(pl.multiple_of(q * bq, bq), bq)

    def fill_vt(jb, c):
        vt_ref[jb] = v_ref[krows(jb), :].T
        return c

    lax.fori_loop(0, nkb, fill_vt, 0)
    acc_ref[...] = jnp.zeros_like(acc_ref)
    carry_ref[...] = jnp.zeros_like(carry_ref)

    def run_pipeline(n_items, first, advance, masked):
        z_ref[...] = jnp.zeros_like(z_ref)
        tail_ref[...] = jnp.zeros_like(tail_ref)
        crow_ref[...] = jnp.full_like(crow_ref, NEG_BIG)
        ring = SB_STAGES

        def earlier(q, j):
            return key_idx + (j - q * kpq) * bk < qry_idx

        def body(t, state):
            nxt, slot_mid, slot_last = state
            slot_w = lax.rem(t, ring)
            slot_mid_z = lax.rem(t + ring - 1, ring)
            slot_last_z = lax.rem(t + ring - 2, ring)

            for u in range(SB_LANES):
                q_d, j_d, _ = slot_last[u]
                arg = z_ref[slot_last_z * SB_LANES + u] + tail_ref[u] + crow_ref[u]
                a = jnp.exp2(arg * LOG2E)
                if masked:
                    a = jnp.where(earlier(q_d, j_d), a, 0.0)
                acc_ref[q_d] += jnp.dot(vt_ref[j_d], a.astype(BF16), preferred_element_type=F32)

            for u in range(SB_LANES):
                q_c, j_c, valid_c = slot_mid[u]
                sp = _softplus(z_ref[slot_mid_z * SB_LANES + u])
                if masked:
                    sp = jnp.where(earlier(q_c, j_c), sp, 0.0)
                tail = jnp.dot(ntri_ref[...], sp.astype(BF16), preferred_element_type=F32)
                tail_ref[u] = tail
                carry = carry_ref[q_c]
                crow_ref[u] = carry + jnp.where(valid_c > 0, 0.0, NEG_BIG)
                carry_ref[q_c] = carry + tail[0:1, :] * valid_c.astype(F32)

            issued = []
            for u in range(SB_LANES):
                q_n, j_n = nxt
                valid_a = (q_n < nqb).astype(jnp.int32)
                q_a = jnp.minimum(q_n, nqb - 1)
                j_a = jnp.minimum(j_n, nkb - 1)
                z_ref[slot_w * SB_LANES + u] = lax.dot_general(
                    k_ref[krows(j_a), :], q_ref[qrows(q_a), :],
                    (((1,), (1,)), ((), ())), preferred_element_type=F32)
                issued.append((q_a, j_a, valid_a))
                nxt = advance(q_n, j_n)
            return (nxt, tuple(issued), slot_mid)

        idle = tuple((jnp.int32(0), jnp.int32(0), jnp.int32(0)) for _ in range(SB_LANES))
        n_iter = -(-n_items // SB_LANES) + SB_STAGES - 1
        lax.fori_loop(0, n_iter, body, (first, idle, idle))

    def advance_diag(q, j):
        wrap = j == q * kpq
        return jnp.where(wrap, q + 1, q), jnp.where(wrap, (q + 2) * kpq - 1, j - 1)

    run_pipeline(nqb * kpq, (jnp.int32(0), jnp.int32(kpq - 1)), advance_diag, masked=True)

    def advance(q, j):
        wrap = j == 0
        return jnp.where(wrap, q + 1, q), jnp.where(wrap, (q + 1) * kpq - 1, j - 1)

    run_pipeline(kpq * nqb * (nqb - 1) // 2, (jnp.int32(1), jnp.int32(kpq - 1)), advance,
                 masked=False)

    def write_out(qb, c):
        o_ref[qrows(qb), :] = acc_ref[qb].T
        return c

    lax.fori_loop(0, nqb, write_out, 0)


def _sb_attention(proj, batch, seq, n_heads, bq, bk):
    m = proj.shape[0]
    dh = SB_HEAD_DIM
    nqb = seq // bq
    nkb = seq // bk
    return pl.pallas_call(
        functools.partial(_sb_attn_kernel, bq=bq, bk=bk, nqb=nqb, nkb=nkb),
        grid=(batch, n_heads),
        in_specs=[
            pl.BlockSpec((seq, dh), lambda b, h: (b, h)),
            pl.BlockSpec((seq, dh), lambda b, h: (b, n_heads + h)),
            pl.BlockSpec((seq, dh), lambda b, h: (b, 2 * n_heads + h)),
        ],
        out_specs=pl.BlockSpec((seq, dh), lambda b, h: (b, h)),
        out_shape=jax.ShapeDtypeStruct((m, n_heads * dh), F32),
        scratch_shapes=[
            pltpu.VMEM((nkb, dh, bk), BF16),
            pltpu.VMEM((nqb, dh, bq), F32),
            pltpu.VMEM((nqb, 1, bq), F32),
            pltpu.VMEM((bk, bk), BF16),
            pltpu.VMEM((SB_STAGES * SB_LANES, bk, bq), F32),
            pltpu.VMEM((SB_LANES, bk, bq), F32),
            pltpu.VMEM((SB_LANES, 1, bq), F32),
        ],
        compiler_params=_params(("arbitrary", "arbitrary")),
        name="sb_attention",
    )(proj, proj, proj)


def _retention_kernel(lg_ref, q_ref, k_ref, v_ref, gate_ref, cos_ref, sin_ref, g_ref,
                      o_ref, state_ref, *, chunk):
    h = pl.program_id(1)
    c_idx = pl.program_id(2)
    half = RET_HEAD_DIM // 2

    @pl.when(c_idx == 0)
    def _():
        state_ref[...] = jnp.zeros_like(state_ref)

    log_gamma = lg_ref[h]
    cos = cos_ref[...]
    sin = sin_ref[...]

    def rotary(x):
        x1, x2 = x[:, :half], x[:, half:]
        return jnp.concatenate([x1 * cos - x2 * sin, x2 * cos + x1 * sin], axis=-1)

    q = rotary(q_ref[...].astype(F32))
    k = rotary(k_ref[...].astype(F32))
    v = v_ref[...]

    row = lax.broadcasted_iota(jnp.int32, (chunk, chunk), 0)
    col = lax.broadcasted_iota(jnp.int32, (chunk, chunk), 1)
    rel = (row - col).astype(F32)
    decay_mask = jnp.where(rel >= 0.0, jnp.exp(log_gamma * jnp.maximum(rel, 0.0)), 0.0)
    idx = lax.broadcasted_iota(jnp.int32, (chunk, 1), 0).astype(F32)
    q_decay = jnp.exp(log_gamma * (idx + 1.0))
    k_decay = jnp.exp(log_gamma * (chunk - 1.0 - idx))
    chunk_decay = jnp.exp(log_gamma * chunk)

    q_b = q.astype(BF16)
    k_b = k.astype(BF16)
    scores = lax.dot_general(q_b, k_b, (((1,), (1,)), ((), ())),
                             preferred_element_type=F32) * decay_mask
    inner = jnp.dot(scores.astype(BF16), v, preferred_element_type=F32)
    state = state_ref[...]
    cross = jnp.dot(q_b, state.astype(BF16), preferred_element_type=F32) * q_decay
    state_ref[...] = state * chunk_decay + lax.dot_general(
        (k * k_decay).astype(BF16), v, (((0,), (0,)), ((), ())),
        preferred_element_type=F32)

    y = inner + cross
    mu = jnp.mean(y, axis=-1, keepdims=True)
    yc = y - mu
    yn = yc * lax.rsqrt(jnp.mean(yc * yc, axis=-1, keepdims=True) + EPS)
    ret = yn * g_ref[...]
    gate = gate_ref[...].astype(F32)
    silu = gate * (1.0 / (1.0 + jnp.exp(-gate)))
    o_ref[...] = (silu * ret).astype(o_ref.dtype)


def _retention(proj, cos, sin, ret_g, batch, seq, n_heads, col0, chunk):
    m = proj.shape[0]
    dh = RET_HEAD_DIM
    nc = seq // chunk
    log_gamma = jnp.log1p(-(2.0 ** (-5.0 - jnp.arange(n_heads, dtype=F32))))

    def col_spec(group):
        base = col0 + group * n_heads
        return pl.BlockSpec((chunk, dh), lambda b, h, c, lg: (b * nc + c, base + h))

    grid_spec = pltpu.PrefetchScalarGridSpec(
        num_scalar_prefetch=1,
        grid=(batch, n_heads, nc),
        in_specs=[
            col_spec(0), col_spec(1), col_spec(2), col_spec(3),
            pl.BlockSpec((chunk, dh // 2), lambda b, h, c, lg: (c, 0)),
            pl.BlockSpec((chunk, dh // 2), lambda b, h, c, lg: (c, 0)),
            pl.BlockSpec((1, dh), lambda b, h, c, lg: (0, h)),
        ],
        out_specs=pl.BlockSpec((chunk, dh), lambda b, h, c, lg: (b * nc + c, h)),
        scratch_shapes=[pltpu.VMEM((dh, dh), F32)],
    )
    return pl.pallas_call(
        functools.partial(_retention_kernel, chunk=chunk),
        grid_spec=grid_spec,
        out_shape=jax.ShapeDtypeStruct((m, n_heads * dh), BF16),
        compiler_params=_params(("arbitrary", "arbitrary", "arbitrary")),
        name="retention",
    )(log_gamma, proj, proj, proj, proj, cos, sin, ret_g.reshape(1, n_heads * dh))


def _outproj_kernel(sb_ref, ret_ref, g_ref, w_ref, x_ref, o_ref, mix_ref):
    sbw = sb_ref.shape[1]

    @pl.when(pl.program_id(1) == 0)
    def _():
        sb = sb_ref[...]
        mix_ref[:, :sbw] = (sb * _rms_scale(sb) * g_ref[...]).astype(mix_ref.dtype)
        mix_ref[:, sbw:] = ret_ref[...]

    o_ref[...] = x_ref[...] + jnp.dot(mix_ref[...], w_ref[...], preferred_element_type=F32)


def _outproj(sb, ret, sb_g, w_bf16, x2d, tm, tn):
    m, d = x2d.shape
    sbw = sb.shape[1]
    rw = ret.shape[1]
    return pl.pallas_call(
        _outproj_kernel,
        grid=(m // tm, d // tn),
        in_specs=[
            pl.BlockSpec((tm, sbw), lambda i, j: (i, 0)),
            pl.BlockSpec((tm, rw), lambda i, j: (i, 0)),
            pl.BlockSpec((1, sbw), lambda i, j: (0, 0)),
            pl.BlockSpec((sbw + rw, tn), lambda i, j: (0, j)),
            pl.BlockSpec((tm, tn), lambda i, j: (i, j)),
        ],
        out_specs=pl.BlockSpec((tm, tn), lambda i, j: (i, j)),
        out_shape=jax.ShapeDtypeStruct((m, d), F32),
        scratch_shapes=[pltpu.VMEM((tm, sbw + rw), BF16)],
        compiler_params=_params(("arbitrary", "arbitrary")),
        name="outproj",
    )(sb, ret, sb_g.reshape(1, sbw), w_bf16, x2d)


def _mlp_kernel(x_ref, g_ref, wu_ref, wd_ref, gf_ref, o_ref, h_ref):
    f = pl.program_id(1)

    @pl.when(f == 0)
    def _():
        x = x_ref[...]
        h_ref[...] = (x * _rms_scale(x) * g_ref[...]).astype(h_ref.dtype)
        o_ref[...] = jnp.zeros_like(o_ref)

    u = jnp.dot(h_ref[...], wu_ref[...], preferred_element_type=F32)
    a = jnp.square(jnp.maximum(u, 0.0)).astype(BF16)
    o_ref[...] += jnp.dot(a, wd_ref[...], preferred_element_type=F32)

    @pl.when(f == pl.num_programs(1) - 1)
    def _():
        x2 = x_ref[...] + o_ref[...]
        o_ref[...] = x2 * _rms_scale(x2) * gf_ref[...]


def _mlp(x1, g, wu_bf16, wd_bf16, gf, tm, tf):
    m, d = x1.shape
    dff = wu_bf16.shape[1]
    return pl.pallas_call(
        _mlp_kernel,
        grid=(m // tm, dff // tf),
        in_specs=[
            pl.BlockSpec((tm, d), lambda i, f: (i, 0), pipeline_mode=pl.Buffered(1)),
            pl.BlockSpec((1, d), lambda i, f: (0, 0)),
            pl.BlockSpec((d, tf), lambda i, f: (0, f)),
            pl.BlockSpec((tf, d), lambda i, f: (f, 0)),
            pl.BlockSpec((1, d), lambda i, f: (0, 0)),
        ],
        out_specs=pl.BlockSpec((tm, d), lambda i, f: (i, 0), pipeline_mode=pl.Buffered(1)),
        out_shape=jax.ShapeDtypeStruct((m, d), F32),
        scratch_shapes=[pltpu.VMEM((tm, d), BF16)],
        compiler_params=_params(("arbitrary", "arbitrary")),
        name="mlp",
    )(x1, g.reshape(1, d), wu_bf16, wd_bf16, gf.reshape(1, d))


def kernel(x, attn_norm_g, w_in, sb_norm_g, ret_norm_g, w_out, mlp_norm_g, w_up, w_down,
           final_norm_g):
    batch, seq, d = x.shape
    depth = w_in.shape[0]
    sb_width = sb_norm_g.shape[-1]
    ret_width = ret_norm_g.shape[-1]
    sb_heads = sb_width // SB_HEAD_DIM
    ret_heads = ret_width // RET_HEAD_DIM
    in_width = w_in.shape[-1]
    assert in_width == 3 * sb_width + 4 * ret_width
    m = batch * seq

    tm = _tile(m, 512)
    sb_bq = _tile(seq, 512)
    sb_bk = _tile(seq, 256)
    chunk = _tile(seq, 256)
    cos, sin = _rope_table(seq, RET_HEAD_DIM // 2)

    colscale = jnp.ones((in_width,), F32)
    colscale = colscale.at[:sb_width].set(SB_HEAD_DIM ** -0.5)
    rk0 = 3 * sb_width + ret_width
    colscale = colscale.at[rk0:rk0 + ret_width].set(RET_HEAD_DIM ** -0.5)

    xf = x.reshape(m, d)
    for l in range(depth):
        proj = _norm_inproj(xf, attn_norm_g[l], w_in[l].astype(BF16), colscale,
                            tm, _tile(in_width, 1024))
        sb = _sb_attention(proj, batch, seq, sb_heads, sb_bq, sb_bk)
        ret = _retention(proj, cos, sin, ret_norm_g[l], batch, seq, ret_heads,
                         3 * sb_width // RET_HEAD_DIM, chunk)
        x1 = _outproj(sb, ret, sb_norm_g[l], w_out[l].astype(BF16), xf, tm, _tile(d, 1024))
        last = l == depth - 1
        assert last, "final norm is fused into the last layer's MLP kernel"
        xf = _mlp(x1, mlp_norm_g[l], w_up[l].astype(BF16), w_down[l].astype(BF16),
                  final_norm_g, tm, _tile(w_up.shape[-1], 512))
    return xf.reshape(batch, seq, d)
```

```python
import functools
import math

import jax
import jax.numpy as jnp
from jax import lax
from jax.experimental import pallas as pl
from jax.experimental.pallas import tpu as pltpu

F32 = jnp.float32
BF16 = jnp.bfloat16

SB_HEAD_DIM = 128
RET_HEAD_DIM = 256
ROPE_BASE = 10000.0
EPS = 1e-6

VMEM_LIMIT_BYTES = 56 * 1024 * 1024


def _params(semantics):
    return pltpu.CompilerParams(dimension_semantics=semantics,
                                vmem_limit_bytes=VMEM_LIMIT_BYTES)


def _tile(n, pref):
    t = pref
    while n % t:
        t //= 2
    return t


def _rms_scale(x):
    return lax.rsqrt(jnp.mean(x * x, axis=-1, keepdims=True) + EPS)


def _rope_kernel(inv_ref, cos_ref, sin_ref):
    rows, half = cos_ref.shape
    pos = lax.broadcasted_iota(jnp.int32, (rows, half), 0) + pl.program_id(0) * rows
    ang = pos.astype(F32) * inv_ref[...]
    cos_ref[...] = jnp.cos(ang)
    sin_ref[...] = jnp.sin(ang)


def _rope_table(seq, half):
    inv_freq = ROPE_BASE ** (-jnp.arange(half, dtype=F32) / half)
    rows = min(seq, 1024)
    return pl.pallas_call(
        _rope_kernel,
        grid=(seq // rows,),
        in_specs=[pl.BlockSpec((1, half), lambda i: (0, 0))],
        out_specs=[pl.BlockSpec((rows, half), lambda i: (i, 0))] * 2,
        out_shape=[jax.ShapeDtypeStruct((seq, half), F32)] * 2,
        compiler_params=_params(("arbitrary",)),
        name="rope_table",
    )(inv_freq.reshape(1, half))


def _norm_inproj_kernel(x_ref, g_ref, w_ref, cs_ref, o_ref, h_ref):
    @pl.when(pl.program_id(1) == 0)
    def _():
        x = x_ref[...]
        h_ref[...] = (x * _rms_scale(x) * g_ref[...]).astype(h_ref.dtype)

    acc = jnp.dot(h_ref[...], w_ref[...], preferred_element_type=F32)
    o_ref[...] = (acc * cs_ref[...]).astype(o_ref.dtype)


def _norm_inproj(x2d, g, w_bf16, colscale, tm, tn):
    m, d = x2d.shape
    n = w_bf16.shape[1]
    return pl.pallas_call(
        _norm_inproj_kernel,
        grid=(m // tm, n // tn),
        in_specs=[
            pl.BlockSpec((tm, d), lambda i, j: (i, 0)),
            pl.BlockSpec((1, d), lambda i, j: (0, 0)),
            pl.BlockSpec((d, tn), lambda i, j: (0, j)),
            pl.BlockSpec((1, tn), lambda i, j: (0, j)),
        ],
        out_specs=pl.BlockSpec((tm, tn), lambda i, j: (i, j)),
        out_shape=jax.ShapeDtypeStruct((m, n), BF16),
        scratch_shapes=[pltpu.VMEM((tm, d), BF16)],
        compiler_params=_params(("arbitrary", "arbitrary")),
        name="norm_inproj",
    )(x2d, g.reshape(1, d), w_bf16, colscale.reshape(1, n))


LOG2E = 1.4426950408889634


def _softplus(z):
    return jnp.maximum(z, 0.0) + jnp.log(1.0 + jnp.exp2(jnp.abs(z) * (-LOG2E)))


NEG_BIG = -1e30
SB_STAGES = 3
SB_LANES = 4


def _sb_attn_kernel(q_ref, k_ref, v_ref, o_ref, vt_ref, acc_ref, carry_ref, ntri_ref,
                    z_ref, tail_ref, crow_ref, *, bq, bk, nqb, nkb):
    kpq = bq // bk
    tri_row = lax.broadcasted_iota(jnp.int32, (bk, bk), 0)
    tri_col = lax.broadcasted_iota(jnp.int32, (bk, bk), 1)
    ntri_ref[...] = jnp.where(tri_col >= tri_row, -1.0, 0.0).astype(BF16)
    key_idx = lax.broadcasted_iota(jnp.int32, (bk, bq), 0)
    qry_idx = lax.broadcasted_iota(jnp.int32, (bk, bq), 1)

    def krows(j):
        return pl.ds(pl.multiple_of(j * bk, bk), bk)

    def qrows(q):
        return pl.ds(pl.multiple_of(q * bq, bq), bq)

    def fill_vt(jb, c):
        vt_ref[jb] = v_ref[krows(jb), :].T
        return c

    lax.fori_loop(0, nkb, fill_vt, 0)
    acc_ref[...] = jnp.zeros_like(acc_ref)
    carry_ref[...] = jnp.zeros_like(carry_ref)

    def run_pipeline(n_items, first, advance, masked):
        z_ref[...] = jnp.zeros_like(z_ref)
        tail_ref[...] = jnp.zeros_like(tail_ref)
        crow_ref[...] = jnp.full_like(crow_ref, NEG_BIG)
        ring = SB_STAGES

        def earlier(q, j):
            return key_idx + (j - q * kpq) * bk < qry_idx

        def body(t, state):
            nxt, slot_mid, slot_last = state
            slot_w = lax.rem(t, ring)
            slot_mid_z = lax.rem(t + ring - 1, ring)
            slot_last_z = lax.rem(t + ring - 2, ring)

            for u in range(SB_LANES):
                q_d, j_d, _ = slot_last[u]
                arg = z_ref[slot_last_z * SB_LANES + u] + tail_ref[u] + crow_ref[u]
                a = jnp.exp2(arg * LOG2E)
                if masked:
                    a = jnp.where(earlier(q_d, j_d), a, 0.0)
                acc_ref[q_d] += jnp.dot(vt_ref[j_d], a.astype(BF16), preferred_element_type=F32)

            for u in range(SB_LANES):
                q_c, j_c, valid_c = slot_mid[u]
                sp = _softplus(z_ref[slot_mid_z * SB_LANES + u])
                if masked:
                    sp = jnp.where(earlier(q_c, j_c), sp, 0.0)
                tail = jnp.dot(ntri_ref[...], sp.astype(BF16), preferred_element_type=F32)
                tail_ref[u] = tail
                carry = carry_ref[q_c]
                crow_ref[u] = carry + jnp.where(valid_c > 0, 0.0, NEG_BIG)
                carry_ref[q_c] = carry + tail[0:1, :] * valid_c.astype(F32)

            issued = []
            for u in range(SB_LANES):
                q_n, j_n = nxt
                valid_a = (q_n < nqb).astype(jnp.int32)
                q_a = jnp.minimum(q_n, nqb - 1)
                j_a = jnp.minimum(j_n, nkb - 1)
                z_ref[slot_w * SB_LANES + u] = lax.dot_general(
                    k_ref[krows(j_a), :], q_ref[qrows(q_a), :],
                    (((1,), (1,)), ((), ())), preferred_element_type=F32)
                issued.append((q_a, j_a, valid_a))
                nxt = advance(q_n, j_n)
            return (nxt, tuple(issued), slot_mid)

        idle = tuple((jnp.int32(0), jnp.int32(0), jnp.int32(0)) for _ in range(SB_LANES))
        n_iter = -(-n_items // SB_LANES) + SB_STAGES - 1
        lax.fori_loop(0, n_iter, body, (first, idle, idle))

    def advance_diag(q, j):
        wrap = j == q * kpq
        return jnp.where(wrap, q + 1, q), jnp.where(wrap, (q + 2) * kpq - 1, j - 1)

    run_pipeline(nqb * kpq, (jnp.int32(0), jnp.int32(kpq - 1)), advance_diag, masked=True)

    def advance(q, j):
        wrap = j == 0
        return jnp.where(wrap, q + 1, q), jnp.where(wrap, (q + 1) * kpq - 1, j - 1)

    run_pipeline(kpq * nqb * (nqb - 1) // 2, (jnp.int32(1), jnp.int32(kpq - 1)), advance,
                 masked=False)

    def write_out(qb, c):
        o_ref[qrows(qb), :] = acc_ref[qb].T
        return c

    lax.fori_loop(0, nqb, write_out, 0)


def _sb_attention(proj, batch, seq, n_heads, bq, bk):
    m = proj.shape[0]
    dh = SB_HEAD_DIM
    nqb = seq // bq
    nkb = seq // bk
    return pl.pallas_call(
        functools.partial(_sb_attn_kernel, bq=bq, bk=bk, nqb=nqb, nkb=nkb),
        grid=(batch, n_heads),
        in_specs=[
            pl.BlockSpec((seq, dh), lambda b, h: (b, h)),
            pl.BlockSpec((seq, dh), lambda b, h: (b, n_heads + h)),
            pl.BlockSpec((seq, dh), lambda b, h: (b, 2 * n_heads + h)),
        ],
        out_specs=pl.BlockSpec((seq, dh), lambda b, h: (b, h)),
        out_shape=jax.ShapeDtypeStruct((m, n_heads * dh), F32),
        scratch_shapes=[
            pltpu.VMEM((nkb, dh, bk), BF16),
            pltpu.VMEM((nqb, dh, bq), F32),
            pltpu.VMEM((nqb, 1, bq), F32),
            pltpu.VMEM((bk, bk), BF16),
            pltpu.VMEM((SB_STAGES * SB_LANES, bk, bq), F32),
            pltpu.VMEM((SB_LANES, bk, bq), F32),
            pltpu.VMEM((SB_LANES, 1, bq), F32),
        ],
        compiler_params=_params(("arbitrary", "arbitrary")),
        name="sb_attention",
    )(proj, proj, proj)


def _retention_kernel(lg_ref, q_ref, k_ref, v_ref, gate_ref, cos_ref, sin_ref, g_ref,
                      o_ref, state_ref, *, chunk):
    h = pl.program_id(1)
    half = RET_HEAD_DIM // 2

    @pl.when(pl.program_id(2) == 0)
    def _():
        state_ref[...] = jnp.zeros_like(state_ref)

    log_gamma = lg_ref[h]
    row = lax.broadcasted_iota(jnp.int32, (chunk, chunk), 0)
    col = lax.broadcasted_iota(jnp.int32, (chunk, chunk), 1)
    rel = (row - col).astype(F32)
    decay_mask = jnp.where(rel >= 0.0, jnp.exp(log_gamma * jnp.maximum(rel, 0.0)), 0.0)
    idx = lax.broadcasted_iota(jnp.int32, (chunk, 1), 0).astype(F32)
    q_decay = jnp.exp(log_gamma * (idx + 1.0))
    k_decay = jnp.exp(log_gamma * (chunk - 1.0 - idx))
    chunk_decay = jnp.exp(log_gamma * chunk)

    for c in range(q_ref.shape[0] // chunk):
        rows = slice(c * chunk, (c + 1) * chunk)
        cos = cos_ref[rows, :]
        sin = sin_ref[rows, :]

        def rotary(x):
            x1, x2 = x[:, :half], x[:, half:]
            return jnp.concatenate([x1 * cos - x2 * sin, x2 * cos + x1 * sin], axis=-1)

        q = rotary(q_ref[rows, :].astype(F32))
        k = rotary(k_ref[rows, :].astype(F32))
        v = v_ref[rows, :]

        q_b = q.astype(BF16)
        k_b = k.astype(BF16)
        scores = lax.dot_general(q_b, k_b, (((1,), (1,)), ((), ())),
                                 preferred_element_type=F32) * decay_mask
        inner = jnp.dot(scores.astype(BF16), v, preferred_element_type=F32)
        state = state_ref[...]
        cross = jnp.dot(q_b, state.astype(BF16), preferred_element_type=F32) * q_decay
        state_ref[...] = state * chunk_decay + lax.dot_general(
            (k * k_decay).astype(BF16), v, (((0,), (0,)), ((), ())),
            preferred_element_type=F32)

        y = inner + cross
        mu = jnp.mean(y, axis=-1, keepdims=True)
        yc = y - mu
        yn = yc * lax.rsqrt(jnp.mean(yc * yc, axis=-1, keepdims=True) + EPS)
        ret = yn * g_ref[...]
        gate = gate_ref[rows, :].astype(F32)
        silu = gate * (1.0 / (1.0 + jnp.exp(-gate)))
        o_ref[rows, :] = (silu * ret).astype(o_ref.dtype)


def _retention(proj, cos, sin, ret_g, batch, seq, n_heads, col0, chunk, rows):
    m = proj.shape[0]
    dh = RET_HEAD_DIM
    ns = seq // rows
    log_gamma = jnp.log1p(-(2.0 ** (-5.0 - jnp.arange(n_heads, dtype=F32))))

    def col_spec(group):
        base = col0 + group * n_heads
        return pl.BlockSpec((rows, dh), lambda b, h, c, lg: (b * ns + c, base + h))

    grid_spec = pltpu.PrefetchScalarGridSpec(
        num_scalar_prefetch=1,
        grid=(batch, n_heads, ns),
        in_specs=[
            col_spec(0), col_spec(1), col_spec(2), col_spec(3),
            pl.BlockSpec((rows, dh // 2), lambda b, h, c, lg: (c, 0)),
            pl.BlockSpec((rows, dh // 2), lambda b, h, c, lg: (c, 0)),
            pl.BlockSpec((1, dh), lambda b, h, c, lg: (0, h)),
        ],
        out_specs=pl.BlockSpec((rows, dh), lambda b, h, c, lg: (b * ns + c, h)),
        scratch_shapes=[pltpu.VMEM((dh, dh), F32)],
    )
    return pl.pallas_call(
        functools.partial(_retention_kernel, chunk=chunk),
        grid_spec=grid_spec,
        out_shape=jax.ShapeDtypeStruct((m, n_heads * dh), BF16),
        compiler_params=_params(("arbitrary", "arbitrary", "arbitrary")),
        name="retention",
    )(log_gamma, proj, proj, proj, proj, cos, sin, ret_g.reshape(1, n_heads * dh))


def _outproj_kernel(sb_ref, ret_ref, g_ref, w_ref, x_ref, o_ref, mix_ref):
    sbw = sb_ref.shape[1]

    @pl.when(pl.program_id(1) == 0)
    def _():
        sb = sb_ref[...]
        mix_ref[:, :sbw] = (sb * _rms_scale(sb) * g_ref[...]).astype(mix_ref.dtype)
        mix_ref[:, sbw:] = ret_ref[...]

    o_ref[...] = x_ref[...] + jnp.dot(mix_ref[...], w_ref[...], preferred_element_type=F32)


def _outproj(sb, ret, sb_g, w_bf16, x2d, tm, tn):
    m, d = x2d.shape
    sbw = sb.shape[1]
    rw = ret.shape[1]
    return pl.pallas_call(
        _outproj_kernel,
        grid=(m // tm, d // tn),
        in_specs=[
            pl.BlockSpec((tm, sbw), lambda i, j: (i, 0)),
            pl.BlockSpec((tm, rw), lambda i, j: (i, 0)),
            pl.BlockSpec((1, sbw), lambda i, j: (0, 0)),
            pl.BlockSpec((sbw + rw, tn), lambda i, j: (0, j)),
            pl.BlockSpec((tm, tn), lambda i, j: (i, j)),
        ],
        out_specs=pl.BlockSpec((tm, tn), lambda i, j: (i, j)),
        out_shape=jax.ShapeDtypeStruct((m, d), F32),
        scratch_shapes=[pltpu.VMEM((tm, sbw + rw), BF16)],
        compiler_params=_params(("arbitrary", "arbitrary")),
        name="outproj",
    )(sb, ret, sb_g.reshape(1, sbw), w_bf16, x2d)


def _mlp_kernel(x_ref, g_ref, wu_ref, wd_ref, gf_ref, o_ref, h_ref):
    f = pl.program_id(1)

    @pl.when(f == 0)
    def _():
        x = x_ref[...]
        h_ref[...] = (x * _rms_scale(x) * g_ref[...]).astype(h_ref.dtype)
        o_ref[...] = jnp.zeros_like(o_ref)

    u = jnp.dot(h_ref[...], wu_ref[...], preferred_element_type=F32)
    a = jnp.square(jnp.maximum(u, 0.0)).astype(BF16)
    o_ref[...] += jnp.dot(a, wd_ref[...], preferred_element_type=F32)

    @pl.when(f == pl.num_programs(1) - 1)
    def _():
        x2 = x_ref[...] + o_ref[...]
        o_ref[...] = x2 * _rms_scale(x2) * gf_ref[...]


def _mlp(x1, g, wu_bf16, wd_bf16, gf, tm, tf):
    m, d = x1.shape
    dff = wu_bf16.shape[1]
    return pl.pallas_call(
        _mlp_kernel,
        grid=(m // tm, dff // tf),
        in_specs=[
            pl.BlockSpec((tm, d), lambda i, f: (i, 0), pipeline_mode=pl.Buffered(1)),
            pl.BlockSpec((1, d), lambda i, f: (0, 0)),
            pl.BlockSpec((d, tf), lambda i, f: (0, f)),
            pl.BlockSpec((tf, d), lambda i, f: (f, 0)),
            pl.BlockSpec((1, d), lambda i, f: (0, 0)),
        ],
        out_specs=pl.BlockSpec((tm, d), lambda i, f: (i, 0), pipeline_mode=pl.Buffered(1)),
        out_shape=jax.ShapeDtypeStruct((m, d), F32),
        scratch_shapes=[pltpu.VMEM((tm, d), BF16)],
        compiler_params=_params(("arbitrary", "arbitrary")),
        name="mlp",
    )(x1, g.reshape(1, d), wu_bf16, wd_bf16, gf.reshape(1, d))


def kernel(x, attn_norm_g, w_in, sb_norm_g, ret_norm_g, w_out, mlp_norm_g, w_up, w_down,
           final_norm_g):
    batch, seq, d = x.shape
    depth = w_in.shape[0]
    sb_width = sb_norm_g.shape[-1]
    ret_width = ret_norm_g.shape[-1]
    sb_heads = sb_width // SB_HEAD_DIM
    ret_heads = ret_width // RET_HEAD_DIM
    in_width = w_in.shape[-1]
    assert in_width == 3 * sb_width + 4 * ret_width
    m = batch * seq

    tm = _tile(m, 512)
    sb_bq = _tile(seq, 512)
    sb_bk = _tile(seq, 256)
    chunk = _tile(seq, 256)
    cos, sin = _rope_table(seq, RET_HEAD_DIM // 2)

    colscale = jnp.ones((in_width,), F32)
    colscale = colscale.at[:sb_width].set(SB_HEAD_DIM ** -0.5)
    rk0 = 3 * sb_width + ret_width
    colscale = colscale.at[rk0:rk0 + ret_width].set(RET_HEAD_DIM ** -0.5)

    xf = x.reshape(m, d)
    for l in range(depth):
        proj = _norm_inproj(xf, attn_norm_g[l], w_in[l].astype(BF16), colscale,
                            tm, _tile(in_width, 1024))
        sb = _sb_attention(proj, batch, seq, sb_heads, sb_bq, sb_bk)
        ret = _retention(proj, cos, sin, ret_norm_g[l], batch, seq, ret_heads,
                         3 * sb_width // RET_HEAD_DIM, chunk, _tile(seq, 8 * chunk))
        x1 = _outproj(sb, ret, sb_norm_g[l], w_out[l].astype(BF16), xf, tm, _tile(d, 1024))
        last = l == depth - 1
        assert last, "final norm is fused into the last layer's MLP kernel"
        xf = _mlp(x1, mlp_norm_g[l], w_up[l].astype(BF16), w_down[l].astype(BF16),
                  final_norm_g, tm, _tile(w_up.shape[-1], 512))
    return xf.reshape(batch, seq, d)
```
